```python
import jax, jax.numpy as jnp
from jax import lax
import numpy as np

D_MODEL = 2048
BATCH = 4
SEQ = 4096
DEPTH = 2

HEAD_DIM = 64
D_MIX = D_MODEL
D_GROUP = D_MIX // 4
N_HEADS = D_GROUP // HEAD_DIM
NORM_EPS = 1e-6
GN_EPS = 64e-5

RET_CHUNK = 128
RET_THETA = 10000.0
RWKV_DECAY_LORA = 96
RWKV_AAA_LORA = 96
RWKV_GATE_LORA = 256
SGU_CHUNK = 128
N_SGU_GROUPS = N_HEADS
DIL_BRANCHES = ((128, 1), (512, 4), (2048, 16))
ROPE_THETA = 500000.0
ROPE_DIM = HEAD_DIM // 4
N_EXPERT_GROUPS = 4
EXPERTS_PER_GROUP = 8
N_EXPERTS = N_EXPERT_GROUPS * EXPERTS_PER_GROUP
TOP_K_IN_GROUP = 2
D_EXPERT = D_MODEL // 4
MOE_BLOCK = 128

A_COLS = 4 * D_GROUP
B_COLS = 3 * D_GROUP + RWKV_DECAY_LORA + RWKV_AAA_LORA + RWKV_GATE_LORA
C_COLS = 2 * D_GROUP
D_COLS = 3 * D_GROUP
D_IN = A_COLS + B_COLS + C_COLS + D_COLS

kernel_name = 'hybrid_parallel_group_encoder'


def rms_norm(x, g):
    x32 = x.astype(jnp.float32)
    y = x32 * lax.rsqrt(jnp.mean(x32 * x32, axis=-1, keepdims=True) + NORM_EPS)
    return (y * g.astype(jnp.float32)).astype(x.dtype)


def group_norm(y, g):
    mu = jnp.mean(y, -1, keepdims=True)
    var = jnp.mean(jnp.square(y - mu), -1, keepdims=True)
    yn = (y - mu) * lax.rsqrt(var + GN_EPS)
    return yn.reshape(*y.shape[:-2], -1) * g.astype(jnp.float32)


def heads(t):
    return t.reshape(*t.shape[:-1], N_HEADS, HEAD_DIM)


def rotary(t, pos, theta, rot_dim):
    half = rot_dim // 2
    inv = theta ** (-jnp.arange(half, dtype=jnp.float32) / half)
    ang = pos.astype(jnp.float32)[..., None] * inv
    cos = jnp.cos(ang)[:, :, None, :]
    sin = jnp.sin(ang)[:, :, None, :]
    t1, t2, rest = t[..., :half], t[..., half:rot_dim], t[..., rot_dim:]
    return jnp.concatenate([t1 * cos - t2 * sin, t2 * cos + t1 * sin, rest], axis=-1)


def retention_one_direction(q, k, v, log_gamma, include_diag):
    b, s, h, dh = q.shape
    c = RET_CHUNK
    n = s // c
    to_chunks = lambda t: t.reshape(b, n, c, h, dh).transpose(1, 0, 3, 2, 4)
    idx = jnp.arange(c, dtype=jnp.float32)
    diff = idx[:, None] - idx[None, :]
    valid = (diff >= 0) if include_diag else (diff > 0)
    lg = log_gamma[:, None, None]
    d_in = jnp.where(valid, jnp.exp(lg * jnp.maximum(diff, 0.0)), 0.0)
    q_dec = jnp.exp(log_gamma[:, None] * (idx + 1.0))[..., None]
    k_dec = jnp.exp(log_gamma[:, None] * (c - 1.0 - idx))[..., None]
    c_dec = jnp.exp(log_gamma * c)[:, None, None]

    def step(state, qkv):
        qc, kc, vc = qkv
        scores = jnp.einsum('bhnd,bhmd->bhnm', qc, kc) * d_in
        out = (jnp.einsum('bhnm,bhme->bhne', scores, vc)
               + jnp.einsum('bhnd,bhde->bhne', qc * q_dec, state))
        state = state * c_dec + jnp.einsum('bhmd,bhme->bhde', kc * k_dec, vc)
        return state, out

    state0 = jnp.zeros((b, h, dh, dh), jnp.float32)
    _, out = lax.scan(step, state0, (to_chunks(q), to_chunks(k), to_chunks(v)))
    return out.transpose(1, 0, 3, 2, 4).reshape(b, s, h, dh)


def retention_mixer(cols, positions, gamma_logit, norm_g):
    q, k, v, g = jnp.split(cols, 4, axis=-1)
    q = rotary(heads(q), positions, RET_THETA, HEAD_DIM)
    k = rotary(heads(k), positions, RET_THETA, HEAD_DIM) * HEAD_DIM ** -0.5
    v = heads(v)
    log_gamma = jax.nn.log_sigmoid(gamma_logit.astype(jnp.float32))
    fwd = retention_one_direction(q, k, v, log_gamma[0], True)
    bwd = retention_one_direction(q[:, ::-1], k[:, ::-1], v[:, ::-1], log_gamma[1], False)[:, ::-1]
    return jax.nn.silu(g) * group_norm(fwd + bwd, norm_g)


def centred_token_shift(t, mu):
    prev = jnp.pad(t[:, :-1], ((0, 0), (1, 0), (0, 0)))
    nxt = jnp.pad(t[:, 1:], ((0, 0), (0, 1), (0, 0)))
    return t + mu[0] * (prev - t) + mu[1] * (nxt - t)


def rwkv7_scan(r, w, k, v, kk, a, reverse):
    b, s, h, dh = r.shape
    tm = lambda t: jnp.moveaxis(t, 1, 0)

    def step(state, inp):
        rt, wt, kt, vt, kkt, at = inp
        sk = jnp.einsum('bhvk,bhk->bhv', state, kkt)
        state = (state * wt[:, :, None, :] - sk[..., None] * (kkt * at)[:, :, None, :]
                 + vt[..., None] * kt[:, :, None, :])
        return state, jnp.einsum('bhvk,bhk->bhv', state, rt)

    state0 = jnp.zeros((b, h, dh, dh), jnp.float32)
    _, y = lax.scan(step, state0, (tm(r), tm(w), tm(k), tm(v), tm(kk), tm(a)), reverse=reverse)
    return jnp.moveaxis(y, 0, 1)


def rwkv7_mixer(cols, mu, w0, w2, a0, a2, g2, k_k, k_a, r_k, ln_g):
    cols = centred_token_shift(cols, mu)
    r, k, v, xw, xa, xg = jnp.split(
        cols, [D_GROUP, 2 * D_GROUP, 3 * D_GROUP, 3 * D_GROUP + RWKV_DECAY_LORA,
               3 * D_GROUP + RWKV_DECAY_LORA + RWKV_AAA_LORA], axis=-1)
    w_lr = jnp.tanh(xw) @ w2
    a_lr = xa @ a2
    g = jax.nn.sigmoid(xg) @ g2
    kk = heads(k * k_k)
    kk = kk * lax.rsqrt(jnp.sum(kk * kk, -1, keepdims=True) + 1e-12)
    rh, vh = heads(r), heads(v)
    y = None
    for direction, rev in ((0, False), (1, True)):
        w_raw = w0[direction] + w_lr
        decay = jnp.exp(-jnp.exp(-jax.nn.softplus(-w_raw) - 0.5))
        a = jax.nn.sigmoid(a0[direction] + a_lr)
        kd = k * (1.0 + (a - 1.0) * k_a)
        yd = rwkv7_scan(rh, heads(decay), heads(kd), vh, kk, heads(a), rev)
        y = yd if y is None else y + yd
    y = group_norm(y, ln_g)
    bonus = jnp.sum(rh * heads(k) * r_k, -1, keepdims=True) * vh
    return (y + bonus.reshape(y.shape)) * g


def sgu_mixer(cols, ln_g, ln_b, w_s, b_s):
    u, v = jnp.split(jax.nn.gelu(cols), 2, axis=-1)
    mu = jnp.mean(v, -1, keepdims=True)
    var = jnp.mean(jnp.square(v - mu), -1, keepdims=True)
    v = (v - mu) * lax.rsqrt(var + NORM_EPS) * ln_g + ln_b
    b, s, _ = v.shape
    n = s // SGU_CHUNK
    vc = v.reshape(b, n, SGU_CHUNK, N_SGU_GROUPS, D_GROUP // N_SGU_GROUPS)
    mixed = (jnp.einsum('gpq,bcqgd->bcpgd', w_s, vc)
             + jnp.transpose(b_s)[None, None, :, :, None])
    return u * mixed.reshape(b, s, D_GROUP)


def dilated_branch(q, k, v, window, dilation):
    b, s, h, dh = q.shape
    r = window // (2 * dilation)
    L = s // dilation
    nb = -(-L // r)
    lp = nb * r

    def subseq(t):
        return t.reshape(b, L, dilation, h, dh).transpose(0, 2, 3, 1, 4)

    qs, ks, vs = subseq(q), subseq(k), subseq(v)
    qb = jnp.pad(qs, ((0, 0), (0, 0), (0, 0), (0, lp - L), (0, 0))).reshape(b, dilation, h, nb, r, dh)

    def band(t):
        tp = jnp.pad(t, ((0, 0), (0, 0), (0, 0), (r, lp - L + r), (0, 0))).reshape(b, dilation, h, nb + 2, r, dh)
        return jnp.concatenate([tp[:, :, :, :-2], tp[:, :, :, 1:-1], tp[:, :, :, 2:]], axis=4)

    kb, vb = band(ks), band(vs)
    qi = jnp.arange(r)[:, None]
    ki = jnp.arange(3 * r)[None, :]
    key_pos = jnp.arange(nb)[:, None, None] * r - r + ki[None]
    valid = (jnp.abs(ki - r - qi)[None] <= r) & (key_pos >= 0) & (key_pos < L)
    scores = jnp.einsum('bghnqe,bghnke->bghnqk', qb, kb)
    scores = jnp.where(valid, scores, -1e30)
    m = jnp.max(scores, -1, keepdims=True)
    p = jnp.exp(scores - m)
    l = jnp.sum(p, -1)
    o = jnp.einsum('bghnqk,bghnke->bghnqe', p, vb)

    def unsub(t):
        t = t.reshape(b, dilation, h, lp, *t.shape[5:])[:, :, :, :L]
        t = jnp.moveaxis(t, 3, 1)
        return t.reshape(b, s, h, *t.shape[4:])

    return unsub(m[..., 0]), unsub(l), unsub(o)


def dilated_attention_mixer(cols, positions):
    q, k, v = jnp.split(cols, 3, axis=-1)
    q = rotary(heads(q), positions, ROPE_THETA, ROPE_DIM) * HEAD_DIM ** -0.5
    k = rotary(heads(k), positions, ROPE_THETA, ROPE_DIM)
    v = heads(v)
    outs = [dilated_branch(q, k, v, w, d) for (w, d) in DIL_BRANCHES]
    m_all = jnp.stack([br[0] for br in outs])
    l_all = jnp.stack([br[1] for br in outs])
    o_all = jnp.stack([br[2] for br in outs])
    wts = jnp.exp(m_all - jnp.max(m_all, 0, keepdims=True))
    num = jnp.sum(wts[..., None] * o_all, 0)
    den = jnp.sum(wts * l_all, 0)
    y = num / den[..., None]
    return y.reshape(*y.shape[:2], D_GROUP)


def hierarchical_moe(h, wg, bg, we, be, w_gate, w_up, w_down):
    b, s, d = h.shape
    t = b * s
    xt = h.reshape(t, d)
    g_prob = jax.nn.softmax((xt @ wg + bg).astype(jnp.float32), axis=-1)
    g_top, g_idx = lax.top_k(g_prob, 1)
    e_logits = (xt @ we + be).astype(jnp.float32).reshape(t, N_EXPERT_GROUPS, EXPERTS_PER_GROUP)
    e_logits = jnp.take_along_axis(e_logits, g_idx[:, :, None], axis=1)[:, 0]
    e_prob = jax.nn.softmax(e_logits, axis=-1)
    e_top, e_local = lax.top_k(e_prob, TOP_K_IN_GROUP)
    gate = g_top * e_top / jnp.sum(e_top, -1, keepdims=True)
    expert = g_idx * EXPERTS_PER_GROUP + e_local
    n_assign = t * TOP_K_IN_GROUP
    a_exp = expert.reshape(n_assign)
    a_tok = jnp.repeat(jnp.arange(t, dtype=jnp.int32), TOP_K_IN_GROUP)
    a_gate = gate.reshape(n_assign)
    order = jnp.argsort(a_exp)
    s_exp, s_tok, s_gate = a_exp[order], a_tok[order], a_gate[order]
    counts = jnp.zeros((N_EXPERTS,), jnp.int32).at[a_exp].add(1)
    starts = jnp.cumsum(counts) - counts
    pcounts = (counts + MOE_BLOCK - 1) // MOE_BLOCK * MOE_BLOCK
    pends = jnp.cumsum(pcounts)
    pstarts = pends - pcounts
    dest = pstarts[s_exp] + jnp.arange(n_assign, dtype=jnp.int32) - starts[s_exp]
    n_blocks = -(-n_assign // MOE_BLOCK) + N_EXPERTS
    p = n_blocks * MOE_BLOCK
    buf_tok = jnp.full((p,), t, jnp.int32).at[dest].set(s_tok)
    buf_gate = jnp.zeros((p,), jnp.float32).at[dest].set(s_gate)
    blk_exp = jnp.minimum(
        jnp.searchsorted(pends, jnp.arange(n_blocks, dtype=jnp.int32) * MOE_BLOCK, side='right'),
        N_EXPERTS - 1)
    x_pad = jnp.concatenate([xt, jnp.zeros((1, d), xt.dtype)], axis=0)

    def expert_block(args):
        tok, e = args
        xb = x_pad[tok]
        hid = jax.nn.silu(xb @ w_gate[e]) * (xb @ w_up[e])
        return hid @ w_down[e]

    y_buf = lax.map(expert_block, (buf_tok.reshape(n_blocks, MOE_BLOCK), blk_exp))
    y = jnp.zeros((t + 1, d), jnp.float32).at[buf_tok].add(
        buf_gate[:, None] * y_buf.reshape(p, d).astype(jnp.float32))
    return y[:t].reshape(b, s, d).astype(h.dtype)


def setup_inputs(seed: int = 0) -> dict:
    key = jax.random.key(seed)
    ks = iter(jax.random.split(key, 40))
    f32 = jnp.float32
    nrm = lambda shape, scale: scale * jax.random.normal(next(ks), shape, f32)
    L = DEPTH
    x = jax.random.normal(next(ks), (BATCH, SEQ, D_MODEL), f32)
    offsets = jax.random.randint(next(ks), (BATCH, 1), 0, 1024, jnp.int32)
    positions = offsets + jnp.arange(SEQ, dtype=jnp.int32)[None, :]
    gamma = 1.0 - 2.0 ** (-5.0 - jnp.arange(N_HEADS, dtype=f32))
    gamma_logit = jnp.log(gamma) - jnp.log1p(-gamma)
    ramp = jnp.linspace(-6.0, 1.0, D_GROUP, dtype=f32)
    return {
        'x': x,
        'positions': positions,
        'norm_attn_g': 1.0 + nrm((L, D_MODEL), 0.05),
        'w_in': nrm((L, D_MODEL, D_IN), D_MODEL ** -0.5),
        'w_out': nrm((L, D_MIX, D_MODEL), D_MIX ** -0.5),
        'ret_gamma_logit': gamma_logit[None, None, :] + nrm((L, 2, N_HEADS), 0.1),
        'ret_norm_g': 1.0 + nrm((L, D_GROUP), 0.05),
        'rwkv_mu': jax.random.uniform(next(ks), (L, 2, B_COLS), f32, 0.0, 0.5),
        'rwkv_w0': ramp + nrm((L, 2, D_GROUP), 0.1),
        'rwkv_w2': nrm((L, RWKV_DECAY_LORA, D_GROUP), RWKV_DECAY_LORA ** -0.5),
        'rwkv_a0': nrm((L, 2, D_GROUP), 0.1),
        'rwkv_a2': nrm((L, RWKV_AAA_LORA, D_GROUP), RWKV_AAA_LORA ** -0.5),
        'rwkv_g2': nrm((L, RWKV_GATE_LORA, D_GROUP), RWKV_GATE_LORA ** -0.5),
        'rwkv_k_k': 0.85 + nrm((L, D_GROUP), 0.05),
        'rwkv_k_a': 1.0 + nrm((L, D_GROUP), 0.05),
        'rwkv_r_k': nrm((L, N_HEADS, HEAD_DIM), 0.1),
        'rwkv_ln_g': 1.0 + nrm((L, D_GROUP), 0.05),
        'sgu_ln_g': 1.0 + nrm((L, D_GROUP), 0.05),
        'sgu_ln_b': nrm((L, D_GROUP), 0.02),
        'sgu_w': nrm((L, N_SGU_GROUPS, SGU_CHUNK, SGU_CHUNK), SGU_CHUNK ** -0.5),
        'sgu_b': 1.0 + nrm((L, N_SGU_GROUPS, SGU_CHUNK), 0.1),
        'norm_ffn_g': 1.0 + nrm((L, D_MODEL), 0.05),
        'router_group_w': nrm((L, D_MODEL, N_EXPERT_GROUPS), D_MODEL ** -0.5),
        'router_group_b': nrm((L, N_EXPERT_GROUPS), 0.01),
        'router_expert_w': nrm((L, D_MODEL, N_EXPERTS), D_MODEL ** -0.5),
        'router_expert_b': nrm((L, N_EXPERTS), 0.01),
        'expert_w_gate': nrm((L, N_EXPERTS, D_MODEL, D_EXPERT), D_MODEL ** -0.5),
        'expert_w_up': nrm((L, N_EXPERTS, D_MODEL, D_EXPERT), D_MODEL ** -0.5),
        'expert_w_down': nrm((L, N_EXPERTS, D_EXPERT, D_MODEL), D_EXPERT ** -0.5),
        'final_norm_g': 1.0 + nrm((D_MODEL,), 0.05),
    }


def reference(x, positions, norm_attn_g, w_in, w_out, ret_gamma_logit, ret_norm_g, rwkv_mu,
              rwkv_w0, rwkv_w2, rwkv_a0, rwkv_a2, rwkv_g2, rwkv_k_k, rwkv_k_a, rwkv_r_k,
              rwkv_ln_g, sgu_ln_g, sgu_ln_b, sgu_w, sgu_b, norm_ffn_g, router_group_w,
              router_group_b, router_expert_w, router_expert_b, expert_w_gate, expert_w_up,
              expert_w_down, final_norm_g):
    for l in range(DEPTH):
        h = rms_norm(x, norm_attn_g[l])
        proj = (h @ w_in[l]).astype(jnp.float32)
        a_cols, b_cols, c_cols, d_cols = jnp.split(
            proj, [A_COLS, A_COLS + B_COLS, A_COLS + B_COLS + C_COLS], axis=-1)
        y_a = retention_mixer(a_cols, positions, ret_gamma_logit[l], ret_norm_g[l])
        y_b = rwkv7_mixer(b_cols, rwkv_mu[l], rwkv_w0[l], rwkv_w2[l], rwkv_a0[l], rwkv_a2[l],
                          rwkv_g2[l], rwkv_k_k[l], rwkv_k_a[l], rwkv_r_k[l], rwkv_ln_g[l])
        y_c = sgu_mixer(c_cols, sgu_ln_g[l], sgu_ln_b[l], sgu_w[l], sgu_b[l])
        y_d = dilated_attention_mixer(d_cols, positions)
        y = jnp.concatenate([y_a, y_b, y_c, y_d], axis=-1).astype(x.dtype)
        x = x + y @ w_out[l]
        h = rms_norm(x, norm_ffn_g[l])
        x = x + hierarchical_moe(h, router_group_w[l], router_group_b[l], router_expert_w[l],
                                 router_expert_b[l], expert_w_gate[l], expert_w_up[l],
                                 expert_w_down[l])
    return rms_norm(x, final_norm_g)
```

```python
import functools
import math

import numpy as np
import jax
import jax.numpy as jnp
from jax import lax
from jax.experimental import pallas as pl
from jax.experimental.pallas import tpu as pltpu

F32 = jnp.float32
BF16 = jnp.bfloat16

LANES = 128
HEAD = 64
PAIR = 2 * HEAD
D_MODEL = 2048
D_GROUP = 512
N_PAIRS = D_GROUP // PAIR
NORM_EPS = 1e-6
GN_EPS = 64e-5
RET_CHUNK = 128
RET_THETA = 10000.0
ROPE_THETA = 500000.0
ROPE_DIM = 16
DECAY_LORA = 96
AAA_LORA = 96
GATE_LORA = 256
SGU_CHUNK = 128
DIL_BRANCHES = ((128, 1), (512, 4), (2048, 16))
DIL_W = 1024
DIL_TQ = 128
N_GROUPS = 4
PER_GROUP = 8
N_EXPERTS = N_GROUPS * PER_GROUP
D_EXPERT = 512
MOE_BLK = 256
RWKV_CHUNK = 64
RWKV_TB = 256
NEG = -1e30

COL_A = 0
COL_B = 2048
COL_C = 4096
COL_D = 5120
N_PROJ = 6656

VMEM_LIMIT = 48 * 1024 * 1024


def _cparams(sem):
    return pltpu.CompilerParams(dimension_semantics=sem, vmem_limit_bytes=VMEM_LIMIT)


def _dot(a, b):
    return jnp.dot(a.astype(BF16), b.astype(BF16), preferred_element_type=F32)


def _dot_nt(a, b):
    return lax.dot_general(a.astype(BF16), b.astype(BF16), (((1,), (1,)), ((), ())),
                           preferred_element_type=F32)


def _dot_tn(a, b):
    return lax.dot_general(a.astype(BF16), b.astype(BF16), (((0,), (0,)), ((), ())),
                           preferred_element_type=F32)


def _split2(x):
    hi = x.astype(BF16)
    lo = (x - hi.astype(F32)).astype(BF16)
    return hi, lo


def _dot_hl(x, m):
    hi, lo = _split2(x)
    return (jnp.dot(hi, m, preferred_element_type=F32) + jnp.dot(lo, m, preferred_element_type=F32))


def _sigmoid(x):
    return 1.0 / (1.0 + jnp.exp(-x))


def _log_sigmoid(x):
    return jnp.minimum(x, 0.0) - jnp.log1p(jnp.exp(-jnp.abs(x)))


def _same_head_mask():
    r = lax.broadcasted_iota(jnp.int32, (PAIR, PAIR), 0) >> 6
    c = lax.broadcasted_iota(jnp.int32, (PAIR, PAIR), 1) >> 6
    return r == c


def _head_block_matrix(value):
    return jnp.where(_same_head_mask(), value, 0.0).astype(BF16)


def _rope(x, cos, sin, half):
    lane = lax.broadcasted_iota(jnp.int32, x.shape, 1)
    first = (lane & (HEAD - 1)) < half
    partner = jnp.where(first, pltpu.roll(x, LANES - half, 1), pltpu.roll(x, half, 1))
    return x * cos + partner * sin


def _group_norm(y, avg, gain):
    mu = _dot_hl(y, avg)
    d = y - mu
    var = _dot_hl(d * d, avg)
    return d * lax.rsqrt(var + GN_EPS) * gain


def _inproj_body(x_ref, g_ref, w_ref, o_ref, h_ref):
    @pl.when(pl.program_id(1) == 0)
    def _():
        x = x_ref[...]
        ms = jnp.mean(x * x, axis=-1, keepdims=True)
        h_ref[...] = (x * lax.rsqrt(ms + NORM_EPS) * g_ref[...]).astype(BF16)

    o_ref[...] = jnp.dot(h_ref[...], w_ref[...], preferred_element_type=F32)


def _inproj(x2d, gain, w_cat):
    t, d = x2d.shape
    n = w_cat.shape[1]
    tm = min(1024, t)
    tn = 512
    return pl.pallas_call(
        _inproj_body,
        grid=(t // tm, n // tn),
        in_specs=[pl.BlockSpec((tm, d), lambda i, j: (i, 0)),
                  pl.BlockSpec((1, d), lambda i, j: (0, 0)),
                  pl.BlockSpec((d, tn), lambda i, j: (0, j))],
        out_specs=pl.BlockSpec((tm, tn), lambda i, j: (i, j)),
        out_shape=jax.ShapeDtypeStruct((t, n), F32),
        scratch_shapes=[pltpu.VMEM((tm, d), BF16)],
        compiler_params=_cparams(("parallel", "arbitrary")),
        name="inproj",
    )(x2d, gain.reshape(1, d), w_cat)


def _rope_tab_body(pos_ref, inv_ref, sgn_ref, cos_ref, sin_ref):
    ang = pos_ref[...] * inv_ref[...]
    cos_ref[...] = jnp.cos(ang)
    sin_ref[...] = jnp.sin(ang) * sgn_ref[...]


def _rope_tables(pos_lanes, inv_lanes, sgn_lanes):
    t = pos_lanes.shape[0]
    tm = min(1024, t)
    spec = pl.BlockSpec((tm, LANES), lambda i: (i, 0))
    row = pl.BlockSpec((1, LANES), lambda i: (0, 0))
    return pl.pallas_call(
        _rope_tab_body,
        grid=(t // tm,),
        in_specs=[spec, row, row],
        out_specs=[spec, spec],
        out_shape=[jax.ShapeDtypeStruct((t, LANES), F32)] * 2,
        compiler_params=_cparams(("parallel",)),
        name="rope_tables",
    )(pos_lanes, inv_lanes, sgn_lanes)


def _rope_lane_consts(theta, rot_dim):
    half = rot_dim // 2
    inv = theta ** (-jnp.arange(half, dtype=F32) / half)
    j = np.arange(LANES) % HEAD
    inv_l = jnp.where(j < rot_dim, inv[j % half], 0.0).astype(F32)
    sgn = np.where(j < half, -1.0, np.where(j < rot_dim, 1.0, 0.0)).astype(np.float32)
    return inv_l.reshape(1, LANES), jnp.asarray(sgn).reshape(1, LANES)


def _ret_body(glog_ref, q_ref, k_ref, v_ref, g_ref, cos_ref, sin_ref, ng_ref, o_ref,
              qs, ks, acc, sf, sb):
    p = pl.program_id(1)
    s_len = q_ref.shape[0]
    c = RET_CHUNK
    n = s_len // c
    lane = lax.broadcasted_iota(jnp.int32, (1, PAIR), 1)
    m0 = lane < HEAD
    ti = lax.broadcasted_iota(jnp.int32, (c, c), 0)
    si = lax.broadcasted_iota(jnp.int32, (c, c), 1)
    diff = (ti - si).astype(F32)
    row = lax.broadcasted_iota(jnp.int32, (c, 1), 0).astype(F32)

    def lane_lg(d):
        return _log_sigmoid(jnp.where(m0, glog_ref[d, 2 * p], glog_ref[d, 2 * p + 1]))

    def decay_matrix(h):
        lf = _log_sigmoid(jnp.full((c, c), glog_ref[0, 2 * p + h], F32))
        lb = _log_sigmoid(jnp.full((c, c), glog_ref[1, 2 * p + h], F32))
        return jnp.exp(jnp.where(diff >= 0, lf, -lb) * diff)

    d0, d1 = decay_matrix(0), decay_matrix(1)
    lgf, lgb = lane_lg(0), lane_lg(1)
    qdec_f = jnp.exp(lgf * (row + 1.0))
    kdec_f = jnp.exp(lgf * (c - 1.0 - row))
    cdec_f = jnp.exp(lgf * c)
    qdec_b = jnp.exp(lgb * (c - row))
    kdec_b = jnp.exp(lgb * row)
    cdec_b = jnp.exp(lgb * c)
    bd = _same_head_mask()
    avg = _head_block_matrix(1.0 / HEAD)

    sf[...] = jnp.zeros_like(sf)
    sb[...] = jnp.zeros_like(sb)

    def fwd_step(ci, carry):
        sl = pl.ds(pl.multiple_of(ci * c, c), c)
        cos, sin = cos_ref[sl, :], sin_ref[sl, :]
        q = _rope(q_ref[sl, :], cos, sin, HEAD // 2)
        k = _rope(k_ref[sl, :], cos, sin, HEAD // 2) * (HEAD ** -0.5)
        v = v_ref[sl, :]
        qs[sl, :] = q
        ks[sl, :] = k
        s0 = _dot_nt(jnp.where(m0, q, 0.0), k) * d0
        s1 = _dot_nt(jnp.where(m0, 0.0, q), k) * d1
        pcat = jnp.concatenate([s0, s1], axis=1)
        vcat = jnp.concatenate([jnp.where(m0, v, 0.0), jnp.where(m0, 0.0, v)], axis=0)
        acc[sl, :] = _dot(pcat, vcat) + _dot(q * qdec_f, sf[...])
        sf[...] = sf[...] * cdec_f + jnp.where(bd, _dot_tn(k * kdec_f, v), 0.0)
        return carry

    lax.fori_loop(0, n, fwd_step, 0)

    def bwd_step(i, carry):
        ci = n - 1 - i
        sl = pl.ds(pl.multiple_of(ci * c, c), c)
        q, k, v = qs[sl, :], ks[sl, :], v_ref[sl, :]
        out = acc[sl, :] + _dot(q * qdec_b, sb[...])
        sb[...] = sb[...] * cdec_b + jnp.where(bd, _dot_tn(k * kdec_b, v), 0.0)
        g = g_ref[sl, :]
        o_ref[sl, :] = (g * _sigmoid(g) * _group_norm(out, avg, ng_ref[...])).astype(o_ref.dtype)
        return carry

    lax.fori_loop(0, n, bwd_step, 0)


def _retention(proj, cos, sin, gamma_logit, norm_g, batch, seq):
    t = batch * seq
    cb = COL_A // LANES

    def col(off):
        return pl.BlockSpec((seq, LANES), lambda b, p, off=off: (b, cb + off + p))

    tab = pl.BlockSpec((seq, LANES), lambda b, p: (b, 0))
    return pl.pallas_call(
        _ret_body,
        grid=(batch, N_PAIRS),
        in_specs=[pl.BlockSpec(memory_space=pltpu.SMEM),
                  col(0), col(N_PAIRS), col(2 * N_PAIRS), col(3 * N_PAIRS), tab, tab,
                  pl.BlockSpec((1, LANES), lambda b, p: (0, p))],
        out_specs=pl.BlockSpec((seq, LANES), lambda b, p: (b, p)),
        out_shape=jax.ShapeDtypeStruct((t, D_GROUP), BF16),
        scratch_shapes=[pltpu.VMEM((seq, LANES), F32), pltpu.VMEM((seq, LANES), F32),
                        pltpu.VMEM((seq, LANES), F32),
                        pltpu.VMEM((PAIR, PAIR), F32), pltpu.VMEM((PAIR, PAIR), F32)],
        compiler_params=_cparams(("parallel", "arbitrary")),
        name="retention",
    )(gamma_logit.astype(F32), proj, proj, proj, proj, cos, sin, norm_g.reshape(1, D_GROUP))


def _rwkv_prep_body(seq, cur_ref, prev_ref, next_ref, mu_ref, w2_ref, a2_ref, g2_ref, o_ref):
    i = pl.program_id(0)
    x = cur_ref[...]
    tm = x.shape[0]
    row = lax.broadcasted_iota(jnp.int32, (tm, 1), 0)
    at_start = (i * tm) % seq == 0
    at_end = ((i + 1) * tm) % seq == 0
    prow = jnp.where(at_start, 0.0, prev_ref[7:8, :])
    nrow = jnp.where(at_end, 0.0, next_ref[0:1, :])
    xp = jnp.where(row == 0, prow, pltpu.roll(x, 1, 0))
    xn = jnp.where(row == tm - 1, nrow, pltpu.roll(x, tm - 1, 0))
    xs = x + mu_ref[0:1, :] * (xp - x) + mu_ref[1:2, :] * (xn - x)
    o_ref[:, 0:1536] = xs[:, 0:1536]
    o_ref[:, 1536:2048] = _dot(jnp.tanh(xs[:, 1536:1664]), w2_ref[...])
    o_ref[:, 2048:2560] = _dot(xs[:, 1664:1792], a2_ref[...])
    o_ref[:, 2560:3072] = _dot(_sigmoid(xs[:, 1792:2048]), g2_ref[...])


def _rwkv_prep(proj, mu_pad, w2_pad, a2_pad, g2, seq):
    t = proj.shape[0]
    tm = min(256, seq)
    cb = COL_B // 2048
    sub = tm // 8
    nsub = t // 8
    return pl.pallas_call(
        functools.partial(_rwkv_prep_body, seq),
        grid=(t // tm,),
        in_specs=[pl.BlockSpec((tm, 2048), lambda i: (i, cb)),
                  pl.BlockSpec((8, 2048), lambda i: (jnp.maximum(i * sub - 1, 0), cb)),
                  pl.BlockSpec((8, 2048), lambda i: (jnp.minimum((i + 1) * sub, nsub - 1), cb)),
                  pl.BlockSpec((2, 2048), lambda i: (0, 0)),
                  pl.BlockSpec((LANES, D_GROUP), lambda i: (0, 0)),
                  pl.BlockSpec((LANES, D_GROUP), lambda i: (0, 0)),
                  pl.BlockSpec((GATE_LORA, D_GROUP), lambda i: (0, 0))],
        out_specs=pl.BlockSpec((tm, 3072), lambda i: (i, 0)),
        out_shape=jax.ShapeDtypeStruct((t, 3072), F32),
        compiler_params=_cparams(("parallel",)),
        name="rwkv_prep",
    )(proj, proj, proj, mu_pad, w2_pad, a2_pad, g2)


def _rwkv_scan_body(reverse, r_ref, k_ref, v_ref, w_ref, a_ref, kk_ref, ka_ref, w0_ref, a0_ref,
                    o_ref, st):
    @pl.when(pl.program_id(2) == 0)
    def _():
        st[...] = jnp.zeros_like(st)

    c = RWKV_CHUNK
    tb = r_ref.shape[0]
    n_chunks = tb // c
    lane = lax.broadcasted_iota(jnp.int32, (1, PAIR), 1)
    m0 = lane < HEAD
    ones_bd = _head_block_matrix(1.0)

    r = r_ref[...]
    k = k_ref[...]
    v = v_ref[...]
    kk = k * kk_ref[...]
    kk = kk * lax.rsqrt(_dot_hl(kk * kk, ones_bd) + 1e-12)
    logw = -math.exp(-0.5) * _sigmoid(w0_ref[...] + w_ref[...])
    a = _sigmoid(a0_ref[...] + a_ref[...])
    kd = k * (1.0 + (a - 1.0) * ka_ref[...])
    b = a * kk

    ri = lax.broadcasted_iota(jnp.int32, (PAIR, PAIR), 0)
    ci = lax.broadcasted_iota(jnp.int32, (PAIR, PAIR), 1)
    same = _same_head_mask()
    tt, ss = ri & (c - 1), ci & (c - 1)
    if reverse:
        strict = same & (ss > tt)
        incl = same & (ss >= tt)
    else:
        strict = same & (ss < tt)
        incl = same & (ss <= tt)
    eye = ri == ci
    t64 = lax.broadcasted_iota(jnp.int32, (c, c), 0)
    s64 = lax.broadcasted_iota(jnp.int32, (c, c), 1)
    tri = jnp.where((s64 >= t64) if reverse else (s64 <= t64), 1.0, 0.0).astype(BF16)

    def stack(z):
        return jnp.concatenate([jnp.where(m0, z, 0.0), jnp.where(m0, 0.0, z)], axis=0)

    units = []
    for g in range(n_chunks):
        sl = slice(g * c, (g + 1) * c)
        lw = logw[sl]
        l1 = lw.astype(BF16)
        rem = lw - l1.astype(F32)
        l2 = rem.astype(BF16)
        l3 = (rem - l2.astype(F32)).astype(BF16)
        cum3 = jnp.dot(tri, jnp.concatenate([l1, l2, l3], axis=1), preferred_element_type=F32)
        cum = cum3[:, 0:PAIR] + cum3[:, PAIR:2 * PAIR] + cum3[:, 2 * PAIR:3 * PAIR]
        gc = jnp.sum(lw, axis=0, keepdims=True)
        e_in = jnp.exp(cum)
        e_ex = jnp.exp(cum - lw)
        e_neg = jnp.exp(-cum)
        e_rem = jnp.exp(gc - cum)
        rh = stack(r[sl] * e_in)
        kh = stack(kk[sl] * e_ex)
        bt = stack(b[sl] * e_neg)
        kt = stack(kd[sl] * e_neg)
        bb = stack(b[sl] * e_rem)
        kb = stack(kd[sl] * e_rem)
        vs = stack(v[sl])

        aa = _dot_nt(jnp.concatenate([kh, rh], axis=0), jnp.concatenate([bt, kt], axis=0))
        a_ab = jnp.where(strict, aa[0:PAIR, 0:PAIR], 0.0)
        a_ak = jnp.where(strict, aa[0:PAIR, PAIR:], 0.0)
        a_rb = jnp.where(incl, aa[PAIR:, 0:PAIR], 0.0)
        a_rk = jnp.where(incl, aa[PAIR:, PAIR:], 0.0)

        pw = a_ab
        tinv = jnp.where(eye, 1.0, 0.0) - a_ab
        for _ in range(5):
            pw = _dot(pw, pw)
            tinv = tinv + _dot(tinv, pw)

        av = _dot(jnp.concatenate([a_ak, a_rk], axis=0), vs)
        tk = _dot(tinv, jnp.concatenate([kh, av[0:PAIR]], axis=1))
        rb = _dot(a_rb, tk)
        rp = rh - rb[:, 0:PAIR]
        y0 = av[PAIR:] - rb[:, PAIR:]
        bt_ku = _dot_tn(bb, tk)
        mt = jnp.where(eye, jnp.exp(gc), 0.0) - bt_ku[:, 0:PAIR]
        nt = _dot_tn(kb, vs) - bt_ku[:, PAIR:]
        units.append((rp, y0, mt, nt))

    order = range(n_chunks - 1, -1, -1) if reverse else range(n_chunks)
    state = st[...]
    for g in order:
        rp, y0, mt, nt = units[g]
        s_hi, s_lo = _split2(state)
        m_hi, m_lo = _split2(mt)
        y = jnp.dot(rp.astype(BF16), s_hi, preferred_element_type=F32) + y0
        o_ref[g * c:(g + 1) * c, :] = y[0:c] + y[c:]
        state = jnp.dot(jnp.concatenate([m_hi, m_lo, m_hi], axis=1),
                        jnp.concatenate([s_hi, s_hi, s_lo], axis=0),
                        preferred_element_type=F32) + nt
    st[...] = state


def _rwkv_scan(bx, kk_p, ka_p, w0_d, a0_d, batch, seq, reverse):
    t = batch * seq
    tb = min(RWKV_TB, seq)
    nb = seq // tb

    def tblk(b, j):
        return b * nb + ((nb - 1 - j) if reverse else j)

    def col(off):
        return pl.BlockSpec((tb, LANES), lambda b, p, j, off=off: (tblk(b, j), off + p))

    prow = pl.BlockSpec((1, LANES), lambda b, p, j: (0, p))
    return pl.pallas_call(
        functools.partial(_rwkv_scan_body, reverse),
        grid=(batch, N_PAIRS, nb),
        in_specs=[col(0), col(N_PAIRS), col(2 * N_PAIRS), col(3 * N_PAIRS), col(4 * N_PAIRS),
                  prow, prow, prow, prow],
        out_specs=pl.BlockSpec((tb, LANES), lambda b, p, j: (tblk(b, j), p)),
        out_shape=jax.ShapeDtypeStruct((t, D_GROUP), F32),
        scratch_shapes=[pltpu.VMEM((PAIR, PAIR), F32)],
        compiler_params=_cparams(("parallel", "parallel", "arbitrary")),
        name="rwkv_scan_bwd" if reverse else "rwkv_scan_fwd",
    )(bx, bx, bx, bx, bx, kk_p.reshape(1, D_GROUP), ka_p.reshape(1, D_GROUP),
      w0_d.reshape(1, D_GROUP), a0_d.reshape(1, D_GROUP))


def _rwkv_out_body(yf_ref, yb_ref, r_ref, k_ref, v_ref, g_ref, rk_ref, ln_ref, o_ref):
    avg = _head_block_matrix(1.0 / HEAD)
    ones_bd = _head_block_matrix(1.0)
    for p in range(N_PAIRS):
        sl = slice(p * LANES, (p + 1) * LANES)
        y = _group_norm(yf_ref[:, sl] + yb_ref[:, sl], avg, ln_ref[:, sl])
        bonus = _dot_hl(r_ref[:, sl] * k_ref[:, sl] * rk_ref[:, sl], ones_bd) * v_ref[:, sl]
        o_ref[:, sl] = ((y + bonus) * g_ref[:, sl]).astype(o_ref.dtype)


def _rwkv_out(yf, yb, bx, r_k, ln_g):
    t = yf.shape[0]
    tm = min(512, t)
    blk = pl.BlockSpec((tm, D_GROUP), lambda i: (i, 0))
    prow = pl.BlockSpec((1, D_GROUP), lambda i: (0, 0))

    def col(c):
        return pl.BlockSpec((tm, D_GROUP), lambda i, c=c: (i, c))

    return pl.pallas_call(
        _rwkv_out_body,
        grid=(t // tm,),
        in_specs=[blk, blk, col(0), col(1), col(2), col(5), prow, prow],
        out_specs=blk,
        out_shape=jax.ShapeDtypeStruct((t, D_GROUP), BF16),
        compiler_params=_cparams(("parallel",)),
        name="rwkv_out",
    )(yf, yb, bx, bx, bx, bx, r_k.reshape(1, D_GROUP), ln_g.reshape(1, D_GROUP))


def _sgu_body(c_ref, lg_ref, lb_ref, w_ref, b_ref, o_ref):
    x = c_ref[...]
    gel = x * (0.5 * (1.0 + jnp.tanh(math.sqrt(2.0 / math.pi) * (x + 0.044715 * (x * x * x)))))
    u, v = gel[:, 0:D_GROUP], gel[:, D_GROUP:]
    mu = jnp.mean(v, axis=-1, keepdims=True)
    dv = v - mu
    var = jnp.mean(dv * dv, axis=-1, keepdims=True)
    v = dv * lax.rsqrt(var + NORM_EPS) * lg_ref[...] + lb_ref[...]
    lane = lax.broadcasted_iota(jnp.int32, (1, PAIR), 1)
    m0 = lane < HEAD
    for ch in range(x.shape[0] // SGU_CHUNK):
        rows = slice(ch * SGU_CHUNK, (ch + 1) * SGU_CHUNK)
        for p in range(N_PAIRS):
            cols = slice(p * LANES, (p + 1) * LANES)
            vp = v[rows, cols]
            rhs = jnp.concatenate([jnp.where(m0, vp, 0.0), jnp.where(m0, 0.0, vp)], axis=0)
            mixed = _dot(w_ref[:, 2 * p * SGU_CHUNK:(2 * p + 2) * SGU_CHUNK], rhs) + b_ref[:, cols]
            o_ref[rows, cols] = (u[rows, cols] * mixed).astype(o_ref.dtype)


def _sgu(proj, ln_g, ln_b, w_cat, b_tab):
    t = proj.shape[0]
    tm = 2 * SGU_CHUNK
    cb = COL_C // 1024
    row = pl.BlockSpec((1, D_GROUP), lambda i: (0, 0))
    return pl.pallas_call(
        _sgu_body,
        grid=(t // tm,),
        in_specs=[pl.BlockSpec((tm, 1024), lambda i: (i, cb)), row, row,
                  pl.BlockSpec((SGU_CHUNK, 8 * SGU_CHUNK), lambda i: (0, 0)),
                  pl.BlockSpec((SGU_CHUNK, D_GROUP), lambda i: (0, 0))],
        out_specs=pl.BlockSpec((tm, D_GROUP), lambda i: (i, 0)),
        out_shape=jax.ShapeDtypeStruct((t, D_GROUP), BF16),
        compiler_params=_cparams(("parallel",)),
        name="sgu",
    )(proj, ln_g.reshape(1, D_GROUP), ln_b.reshape(1, D_GROUP), w_cat, b_tab)


def _dil_bias_table():
    qi = np.arange(DIL_TQ)[:, None]
    col = np.arange(DIL_TQ + 2 * DIL_W)[None, :]
    delta = col - DIL_W - qi
    count = np.zeros(delta.shape, np.float64)
    for window, dil in DIL_BRANCHES:
        radius = window // (2 * dil)
        count += (delta % dil == 0) & (np.abs(delta) <= radius * dil)
    return np.where(count > 0, np.log(np.maximum(count, 1.0)), NEG).astype(np.float32)


def _dil_body(q_ref, k_ref, v_ref, cos_ref, sin_ref, bias_ref, o_ref, kp, vp):
    i = pl.program_id(2)
    s_len = k_ref.shape[0]
    nk = DIL_TQ + 2 * DIL_W
    half = ROPE_DIM // 2

    @pl.when(i == 0)
    def _():
        zeros = jnp.zeros((DIL_W, LANES), BF16)
        kp[0:DIL_W, :] = zeros
        vp[0:DIL_W, :] = zeros
        kp[DIL_W + s_len:, :] = zeros
        vp[DIL_W + s_len:, :] = zeros
        step = min(512, s_len)
        for j in range(s_len // step):
            sl = slice(j * step, (j + 1) * step)
            dst = slice(DIL_W + j * step, DIL_W + (j + 1) * step)
            kp[dst, :] = _rope(k_ref[sl, :], cos_ref[sl, :], sin_ref[sl, :], half).astype(BF16)
            vp[dst, :] = v_ref[sl, :].astype(BF16)

    row0 = pl.multiple_of(i * DIL_TQ, DIL_TQ)
    qsl = pl.ds(row0, DIL_TQ)
    q = _rope(q_ref[...], cos_ref[qsl, :], sin_ref[qsl, :], half) * (HEAD ** -0.5)
    kb = kp[pl.ds(row0, nk), :]
    vb = vp[pl.ds(row0, nk), :]
    kpos = row0 - DIL_W + lax.broadcasted_iota(jnp.int32, (1, nk), 1)
    bias = bias_ref[...] + jnp.where((kpos >= 0) & (kpos < s_len), 0.0, NEG)
    lane = lax.broadcasted_iota(jnp.int32, (1, PAIR), 1)
    m0 = lane < HEAD
    outs = []
    for h in range(2):
        qh = jnp.where(m0, q, 0.0) if h == 0 else jnp.where(m0, 0.0, q)
        s = _dot_nt(qh, kb) + bias
        m = jnp.max(s, axis=-1, keepdims=True)
        e = jnp.exp(s - m)
        den = jnp.sum(e, axis=-1, keepdims=True)
        outs.append(_dot(e, vb) / den)
    o_ref[...] = jnp.where(m0, outs[0], outs[1]).astype(o_ref.dtype)


def _dilated(proj, cos, sin, bias, batch, seq):
    t = batch * seq
    nq = seq // DIL_TQ
    cb = COL_D // LANES
    nk = DIL_TQ + 2 * DIL_W
    full = lambda off: pl.BlockSpec((seq, LANES), lambda b, p, i, off=off: (b, cb + off + p))
    tab = pl.BlockSpec((seq, LANES), lambda b, p, i: (b, 0))
    return pl.pallas_call(
        _dil_body,
        grid=(batch, N_PAIRS, nq),
        in_specs=[pl.BlockSpec((DIL_TQ, LANES), lambda b, p, i: (b * nq + i, cb + p)),
                  full(N_PAIRS), full(2 * N_PAIRS), tab, tab,
                  pl.BlockSpec((DIL_TQ, nk), lambda b, p, i: (0, 0))],
        out_specs=pl.BlockSpec((DIL_TQ, LANES), lambda b, p, i: (b * nq + i, p)),
        out_shape=jax.ShapeDtypeStruct((t, D_GROUP), BF16),
        scratch_shapes=[pltpu.VMEM((seq + 2 * DIL_W, LANES), BF16),
                        pltpu.VMEM((seq + 2 * DIL_W, LANES), BF16)],
        compiler_params=_cparams(("parallel", "parallel", "arbitrary")),
        name="dilated_attn",
    )(proj, proj, proj, cos, sin, bias)


def _outproj_body(x_ref, ya_ref, yb_ref, yc_ref, yd_ref, w_ref, ng_ref, wrh_ref, wrl_ref, rb_ref,
                  x1_ref, route_ref):
    acc = x_ref[...]
    for idx, y_ref in enumerate((ya_ref, yb_ref, yc_ref, yd_ref)):
        acc = acc + jnp.dot(y_ref[...], w_ref[idx * D_GROUP:(idx + 1) * D_GROUP, :],
                            preferred_element_type=F32)
    x1_ref[...] = acc
    ms = jnp.mean(acc * acc, axis=-1, keepdims=True)
    h = acc * lax.rsqrt(ms + NORM_EPS) * ng_ref[...]
    hh, hl = _split2(h)
    logits = (jnp.dot(hh, wrh_ref[...], preferred_element_type=F32)
              + jnp.dot(hl, wrh_ref[...], preferred_element_type=F32)
              + jnp.dot(hh, wrl_ref[...], preferred_element_type=F32)) + rb_ref[...]

    lane = lax.broadcasted_iota(jnp.int32, logits.shape, 1)
    lanef = lane.astype(F32)
    big = float(LANES)

    def rmax(z):
        return jnp.max(z, axis=-1, keepdims=True)

    def first_lane(mask):
        return jnp.min(jnp.where(mask, lanef, big), axis=-1, keepdims=True)

    gl = jnp.where(lane < N_GROUPS, logits, NEG)
    gmax = rmax(gl)
    g_top = 1.0 / jnp.sum(jnp.exp(gl - gmax), axis=-1, keepdims=True)
    g_idx = first_lane(gl == gmax)
    lo = N_GROUPS + PER_GROUP * g_idx
    el = jnp.where((lanef >= lo) & (lanef < lo + PER_GROUP), logits, NEG)
    emax = rmax(el)
    esum = jnp.sum(jnp.exp(el - emax), axis=-1, keepdims=True)
    e1 = first_lane(el == emax)
    el2 = jnp.where(lanef == e1, NEG, el)
    emax2 = rmax(el2)
    e2 = first_lane(el2 == emax2)
    p1 = 1.0 / esum
    p2 = jnp.exp(emax2 - emax) / esum
    scale = g_top / (p1 + p2)
    route_ref[...] = jnp.where(lane == 0, e1 - N_GROUPS,
                               jnp.where(lane == 1, e2 - N_GROUPS,
                                         jnp.where(lane == 2, p1 * scale,
                                                   jnp.where(lane == 3, p2 * scale, 0.0))))


def _outproj(x2d, ys, w_out, ng, wr_hi, wr_lo, rbias):
    t, d = x2d.shape
    tm = min(256, t)
    xb = pl.BlockSpec((tm, d), lambda i: (i, 0))
    yb = pl.BlockSpec((tm, D_GROUP), lambda i: (i, 0))
    const = lambda shape: pl.BlockSpec(shape, lambda i: (0, 0))
    return pl.pallas_call(
        _outproj_body,
        grid=(t // tm,),
        in_specs=[xb, yb, yb, yb, yb, const((d, d)), const((1, d)), const((d, LANES)),
                  const((d, LANES)), const((1, LANES))],
        out_specs=[xb, pl.BlockSpec((tm, LANES), lambda i: (i, 0))],
        out_shape=[jax.ShapeDtypeStruct((t, d), F32), jax.ShapeDtypeStruct((t, LANES), F32)],
        compiler_params=_cparams(("parallel",)),
        name="outproj_router",
    )(x2d, *ys, w_out, ng.reshape(1, d), wr_hi, wr_lo, rbias)


def _moe_body(be_ref, nu_ref, src_ref, srcn_ref, dst_ref, gate_ref, x_hbm, ng_ref, wg_ref, wu_ref,
              wd_ref, out_hbm, xg, yo, gsem, ssem):
    i = pl.program_id(0)
    n_used = nu_ref[0]
    slot = i % 2

    def gather_copy(idx, r, s):
        return pltpu.make_async_copy(x_hbm.at[pl.ds(idx, 1), :], xg.at[s, pl.ds(r, 1), :], gsem.at[s])

    def scatter_copy(idx, r, s):
        return pltpu.make_async_copy(yo.at[s, pl.ds(r, 1), :], out_hbm.at[pl.ds(idx, 1), :], ssem.at[s])

    def for_rows(fn):
        def body(r, carry):
            fn(r)
            return carry
        lax.fori_loop(0, MOE_BLK, body, 0, unroll=8)

    @pl.when(i == 0)
    def _():
        for_rows(lambda r: gather_copy(src_ref[0, 0, r], r, 0).start())

    @pl.when(i + 1 < n_used)
    def _():
        for_rows(lambda r: gather_copy(srcn_ref[0, 0, r], r, 1 - slot).start())

    @pl.when(i < n_used)
    def _():
        for_rows(lambda r: gather_copy(0, r, slot).wait())

        @pl.when(i >= 2)
        def _():
            for_rows(lambda r: scatter_copy(0, r, slot).wait())

        x = xg[slot]
        ms = jnp.mean(x * x, axis=-1, keepdims=True)
        h = (x * lax.rsqrt(ms + NORM_EPS) * ng_ref[...]).astype(BF16)
        hg = jnp.dot(h, wg_ref[...], preferred_element_type=F32)
        hu = jnp.dot(h, wu_ref[...], preferred_element_type=F32)
        hid = (hg * _sigmoid(hg) * hu).astype(BF16)
        yo[slot] = jnp.dot(hid, wd_ref[...], preferred_element_type=F32) * gate_ref[...]
        for_rows(lambda r: scatter_copy(dst_ref[0, 0, r], r, slot).start())

    @pl.when(i == n_used - 1)
    def _():
        @pl.when(i >= 1)
        def _():
            for_rows(lambda r: scatter_copy(0, r, 1 - slot).wait())

        for_rows(lambda r: scatter_copy(0, r, slot).wait())


def _moe(x1, ng, wg, wu, wd, blk_exp, n_used, src, dst, gate, n_out_rows):
    t, d = x1.shape
    n_blocks = src.shape[0]
    idx_spec = lambda fn: pl.BlockSpec((1, 1, MOE_BLK), fn, memory_space=pltpu.SMEM)
    grid_spec = pltpu.PrefetchScalarGridSpec(
        num_scalar_prefetch=2,
        grid=(n_blocks,),
        in_specs=[idx_spec(lambda i, be, nu: (i, 0, 0)),
                  idx_spec(lambda i, be, nu: (jnp.minimum(i + 1, n_blocks - 1), 0, 0)),
                  idx_spec(lambda i, be, nu: (i, 0, 0)),
                  pl.BlockSpec((MOE_BLK, 1), lambda i, be, nu: (i, 0)),
                  pl.BlockSpec(memory_space=pl.ANY),
                  pl.BlockSpec((1, d), lambda i, be, nu: (0, 0)),
                  pl.BlockSpec((None, d, D_EXPERT), lambda i, be, nu: (be[i], 0, 0)),
                  pl.BlockSpec((None, d, D_EXPERT), lambda i, be, nu: (be[i], 0, 0)),
                  pl.BlockSpec((None, D_EXPERT, d), lambda i, be, nu: (be[i], 0, 0))],
        out_specs=pl.BlockSpec(memory_space=pl.ANY),
        scratch_shapes=[pltpu.VMEM((2, MOE_BLK, d), F32), pltpu.VMEM((2, MOE_BLK, d), F32),
                        pltpu.SemaphoreType.DMA((2,)), pltpu.SemaphoreType.DMA((2,))],
    )
    return pl.pallas_call(
        _moe_body,
        grid_spec=grid_spec,
        out_shape=jax.ShapeDtypeStruct((n_out_rows, d), F32),
        compiler_params=_cparams(("arbitrary",)),
        name="moe_experts",
    )(blk_exp, n_used, src, src, dst, gate, x1, ng.reshape(1, d), wg, wu, wd)


def _route_plan(route, t):
    n_assign = 2 * t
    n_blocks = n_assign // MOE_BLK + N_EXPERTS
    p_rows = n_blocks * MOE_BLK
    a_exp = route[:, 0:2].astype(jnp.int32).reshape(n_assign)
    a_gate = route[:, 2:4].reshape(n_assign)
    onehot = (a_exp[:, None] == jnp.arange(N_EXPERTS, dtype=jnp.int32)[None, :]).astype(jnp.int32)
    ranks = jnp.cumsum(onehot, axis=0) - onehot
    counts = jnp.sum(onehot, axis=0)
    rank = jnp.sum(ranks * onehot, axis=1)
    pcounts = (counts + MOE_BLK - 1) // MOE_BLK * MOE_BLK
    pends = jnp.cumsum(pcounts)
    pstarts = pends - pcounts
    dest = pstarts[a_exp] + rank
    a_tok = jnp.arange(n_assign, dtype=jnp.int32) // 2
    a_slot = jnp.arange(n_assign, dtype=jnp.int32) % 2
    spill = n_assign + jnp.arange(p_rows, dtype=jnp.int32) % (2 * MOE_BLK)
    src = jnp.zeros((p_rows,), jnp.int32).at[dest].set(a_tok)
    dst = spill.at[dest].set(a_slot * t + a_tok)
    gate = jnp.zeros((p_rows,), F32).at[dest].set(a_gate)
    n_used = (pends[-1] // MOE_BLK).astype(jnp.int32)
    blk = jnp.arange(n_blocks, dtype=jnp.int32)
    blk_exp = jnp.minimum(jnp.searchsorted(pends, jnp.minimum(blk, n_used - 1) * MOE_BLK, side='right'),
                          N_EXPERTS - 1).astype(jnp.int32)
    return (blk_exp, n_used.reshape(1), src.reshape(n_blocks, 1, MOE_BLK),
            dst.reshape(n_blocks, 1, MOE_BLK), gate.reshape(p_rows, 1), n_assign + 2 * MOE_BLK)


def _combine_body(final, x_ref, o0_ref, o1_ref, g_ref, o_ref):
    x = x_ref[...] + o0_ref[...] + o1_ref[...]
    if final:
        ms = jnp.mean(x * x, axis=-1, keepdims=True)
        x = x * lax.rsqrt(ms + NORM_EPS) * g_ref[...]
    o_ref[...] = x


def _combine(x1, out2, gain, final):
    t, d = x1.shape
    tm = min(512, t)
    nt = t // tm
    return pl.pallas_call(
        functools.partial(_combine_body, final),
        grid=(nt,),
        in_specs=[pl.BlockSpec((tm, d), lambda i: (i, 0)),
                  pl.BlockSpec((tm, d), lambda i: (i, 0)),
                  pl.BlockSpec((tm, d), lambda i: (i + nt, 0)),
                  pl.BlockSpec((1, d), lambda i: (0, 0))],
        out_specs=pl.BlockSpec((tm, d), lambda i: (i, 0)),
        out_shape=jax.ShapeDtypeStruct((t, d), F32),
        compiler_params=_cparams(("parallel",)),
        name="combine_final" if final else "combine",
    )(x1, out2, out2, gain.reshape(1, d))


def _pad_cols(w, width):
    return jnp.pad(w, ((0, 0), (0, width - w.shape[1])))


def _layer_weights(w_in_l):
    b0 = 4 * D_GROUP
    xw0 = b0 + 3 * D_GROUP
    xa0 = xw0 + DECAY_LORA
    xg0 = xa0 + AAA_LORA
    c0 = xg0 + GATE_LORA
    pieces = [w_in_l[:, 0:b0], w_in_l[:, b0:xw0], _pad_cols(w_in_l[:, xw0:xa0], LANES),
              _pad_cols(w_in_l[:, xa0:xg0], LANES), w_in_l[:, xg0:c0], w_in_l[:, c0:]]
    return jnp.concatenate(pieces, axis=1).astype(BF16)


def _mixers(proj, tabs, l, batch, seq, ret_gamma_logit, ret_norm_g, rwkv_mu, rwkv_w0, rwkv_w2, rwkv_a0,
            rwkv_a2, rwkv_g2, rwkv_k_k, rwkv_k_a, rwkv_r_k, rwkv_ln_g, sgu_ln_g, sgu_ln_b, sgu_w, sgu_b):
    cos_a, sin_a, cos_d, sin_d, dil_bias = tabs
    y_a = _retention(proj, cos_a, sin_a, ret_gamma_logit[l], ret_norm_g[l], batch, seq)

    mu = rwkv_mu[l]
    n_rkv = 3 * D_GROUP
    mu_pad = jnp.concatenate([mu[:, 0:n_rkv], _pad_cols(mu[:, n_rkv:n_rkv + DECAY_LORA], LANES),
                              _pad_cols(mu[:, n_rkv + DECAY_LORA:n_rkv + DECAY_LORA + AAA_LORA], LANES),
                              mu[:, n_rkv + DECAY_LORA + AAA_LORA:]], axis=1)
    w2_pad = jnp.pad(rwkv_w2[l], ((0, LANES - DECAY_LORA), (0, 0))).astype(BF16)
    a2_pad = jnp.pad(rwkv_a2[l], ((0, LANES - AAA_LORA), (0, 0))).astype(BF16)
    bx = _rwkv_prep(proj, mu_pad, w2_pad, a2_pad, rwkv_g2[l].astype(BF16), seq)
    y_f = _rwkv_scan(bx, rwkv_k_k[l], rwkv_k_a[l], rwkv_w0[l, 0], rwkv_a0[l, 0], batch, seq, False)
    y_r = _rwkv_scan(bx, rwkv_k_k[l], rwkv_k_a[l], rwkv_w0[l, 1], rwkv_a0[l, 1], batch, seq, True)
    y_b = _rwkv_out(y_f, y_r, bx, rwkv_r_k[l].reshape(D_GROUP), rwkv_ln_g[l])

    w_cat = jnp.transpose(sgu_w[l], (1, 0, 2)).reshape(SGU_CHUNK, 8 * SGU_CHUNK).astype(BF16)
    b_tab = jnp.repeat(jnp.transpose(sgu_b[l]), HEAD, axis=1)
    y_c = _sgu(proj, sgu_ln_g[l], sgu_ln_b[l], w_cat, b_tab)

    y_d = _dilated(proj, cos_d, sin_d, dil_bias, batch, seq)
    return y_a, y_b, y_c, y_d


def kernel(x, positions, norm_attn_g, w_in, w_out, ret_gamma_logit, ret_norm_g, rwkv_mu, rwkv_w0, rwkv_w2, rwkv_a0, rwkv_a2, rwkv_g2, rwkv_k_k, rwkv_k_a, rwkv_r_k, rwkv_ln_g, sgu_ln_g, sgu_ln_b, sgu_w, sgu_b, norm_ffn_g, router_group_w, router_group_b, router_expert_w, router_expert_b, expert_w_gate, expert_w_up, expert_w_down, final_norm_g):
    batch, seq, d = x.shape
    t = batch * seq
    depth = w_in.shape[0]
    xc = x.reshape(t, d)

    pos_lanes = jnp.broadcast_to(positions.astype(F32).reshape(t, 1), (t, LANES))
    cos_a, sin_a = _rope_tables(pos_lanes, *_rope_lane_consts(RET_THETA, HEAD))
    cos_d, sin_d = _rope_tables(pos_lanes, *_rope_lane_consts(ROPE_THETA, ROPE_DIM))
    tabs = (cos_a, sin_a, cos_d, sin_d, jnp.asarray(_dil_bias_table()))

    for l in range(depth):
        proj = _inproj(xc, norm_attn_g[l], _layer_weights(w_in[l]))
        ys = _mixers(proj, tabs, l, batch, seq, ret_gamma_logit, ret_norm_g, rwkv_mu, rwkv_w0, rwkv_w2,
                     rwkv_a0, rwkv_a2, rwkv_g2, rwkv_k_k, rwkv_k_a, rwkv_r_k, rwkv_ln_g, sgu_ln_g,
                     sgu_ln_b, sgu_w, sgu_b)
        wr = jnp.pad(jnp.concatenate([router_group_w[l], router_expert_w[l]], axis=1),
                     ((0, 0), (0, LANES - N_GROUPS - N_EXPERTS)))
        wr_hi = wr.astype(BF16)
        wr_lo = (wr - wr_hi.astype(F32)).astype(BF16)
        rbias = jnp.pad(jnp.concatenate([router_group_b[l], router_expert_b[l]]),
                        (0, LANES - N_GROUPS - N_EXPERTS)).reshape(1, LANES)
        x1, route = _outproj(xc, ys, w_out[l].astype(BF16), norm_ffn_g[l], wr_hi, wr_lo, rbias)
        blk_exp, n_used, src, dst, gate, n_rows = _route_plan(route, t)
        out2 = _moe(x1, norm_ffn_g[l], expert_w_gate[l].astype(BF16), expert_w_up[l].astype(BF16),
                    expert_w_down[l].astype(BF16), blk_exp, n_used, src, dst, gate, n_rows)
        xc = _combine(x1, out2, final_norm_g, l == depth - 1)
    return xc.reshape(batch, seq, d)
```
